```python
import math
import jax, jax.numpy as jnp
from jax import lax
import numpy as np

D_MODEL = 1024
BATCH = 16
SEQ = 2048
DEPTH = 1

GRID_W = 64
WIN_H = 8
WIN_W = 16
ATTN_HEADS = 8
HEAD_DIM = 64
ATTN_WIDTH = ATTN_HEADS * HEAD_DIM
CONV_WIDTH = D_MODEL - ATTN_WIDTH
CONV_K = 31
MIX_WIDTH = ATTN_WIDTH + CONV_WIDTH
IN_WIDTH = 3 * ATTN_WIDTH + 2 * CONV_WIDTH
PEER_HEADS = 8
PEER_QDIM = 256
PEER_HALF = PEER_QDIM // 2
N_KEYS = 128
N_EXPERTS = N_KEYS * N_KEYS
PEER_TOPK = 16
PEER_CHUNK = 128
LN_EPS = 1e-5
DEEPNORM_ALPHA = (2.0 * DEPTH) ** 0.25
DEEPNORM_BETA = (8.0 * DEPTH) ** -0.25
MASK_VALUE = -1e30

kernel_name = "hybrid_natten_conformer_peer_block"


def _layer_norm(x, gain=None, bias=None):
    xf = x.astype(jnp.float32)
    mu = jnp.mean(xf, axis=-1, keepdims=True)
    var = jnp.mean(jnp.square(xf - mu), axis=-1, keepdims=True)
    y = (xf - mu) * lax.rsqrt(var + LN_EPS)
    if gain is not None:
        y = y * gain.astype(jnp.float32) + bias.astype(jnp.float32)
    return y.astype(x.dtype)


def _neighbourhood_attention(q, k, v, rpb):
    b, s, h, dh = q.shape
    rows = s // GRID_W
    kh = min(WIN_H, rows)
    ncb = GRID_W // WIN_W
    kcw = 2 * WIN_W
    qg = q.reshape(b, rows, GRID_W, h, dh).transpose(1, 0, 3, 2, 4)
    kg = k.reshape(b, rows, GRID_W, h, dh).transpose(0, 3, 1, 2, 4)
    vg = v.reshape(b, rows, GRID_W, h, dh).transpose(0, 3, 1, 2, 4)
    qcol = jnp.arange(GRID_W)
    col_start = jnp.clip(qcol - WIN_W // 2, 0, GRID_W - WIN_W)
    blk_start = jnp.clip(jnp.arange(ncb) * WIN_W - WIN_W // 2, 0, GRID_W - kcw)
    key_col = blk_start[:, None] + jnp.arange(kcw)[None, :]
    qcol_b = qcol.reshape(ncb, WIN_W)
    cs_b = col_start.reshape(ncb, WIN_W)
    kc = key_col[:, None, :]
    col_valid = (kc >= cs_b[:, :, None]) & (kc < cs_b[:, :, None] + WIN_W)
    dc_idx = jnp.clip(kc - qcol_b[:, :, None] + WIN_W - 1, 0, 2 * WIN_W - 2)
    kg_c = kg[:, :, :, key_col]
    vg_c = vg[:, :, :, key_col]
    scale = dh ** -0.5

    def row_step(args):
        r, q_row = args
        rs = jnp.clip(r - kh // 2, 0, rows - kh)
        k_blk = lax.dynamic_slice_in_dim(kg_c, rs, kh, axis=2)
        v_blk = lax.dynamic_slice_in_dim(vg_c, rs, kh, axis=2)
        qb = q_row.reshape(b, h, ncb, WIN_W, dh)
        sc = jnp.einsum('bhnqd,bhinkd->bhnqik', qb, k_blk).astype(jnp.float32) * scale
        dr_idx = rs + jnp.arange(kh) - r + WIN_H - 1
        bias = rpb[:, dr_idx[None, None, :, None], dc_idx[:, :, None, :]]
        sc = sc + bias[None].astype(jnp.float32)
        sc = jnp.where(col_valid[:, :, None, :], sc, MASK_VALUE)
        p = jax.nn.softmax(sc, axis=(-2, -1))
        out = jnp.einsum('bhnqik,bhinkd->bhnqd', p.astype(v_blk.dtype), v_blk)
        return out.reshape(b, h, GRID_W, dh)

    out = lax.map(row_step, (jnp.arange(rows), qg))
    return out.transpose(1, 0, 3, 2, 4).reshape(b, s, h * dh)


def _conformer_conv(a, g, conv_w, conv_b, ln_g, ln_b):
    u = a * jax.nn.sigmoid(g)
    kern = conv_w[:, None, :].astype(u.dtype)
    y = lax.conv_general_dilated(u, kern, window_strides=(1,),
                                 padding=[(CONV_K // 2, CONV_K // 2)],
                                 dimension_numbers=('NWC', 'WIO', 'NWC'),
                                 feature_group_count=CONV_WIDTH) + conv_b
    y = _layer_norm(y, ln_g, ln_b)
    return jax.nn.silu(y)


def _peer(h, w_query, sub_keys, expert_u, expert_v):
    b, s, d = h.shape
    chunks = h.reshape((b * s) // PEER_CHUNK, PEER_CHUNK, d)

    def chunk_step(xc):
        t = xc.shape[0]
        q = (xc @ w_query).reshape(t, PEER_HEADS, 2, PEER_HALF)
        sc = jnp.einsum('thpc,hpnc->thpn', q, sub_keys).astype(jnp.float32)
        top_v, top_i = lax.top_k(sc, PEER_TOPK)
        cand = (top_v[:, :, 0, :, None] + top_v[:, :, 1, None, :]).reshape(t, PEER_HEADS, PEER_TOPK * PEER_TOPK)
        best_v, best_f = lax.top_k(cand, PEER_TOPK)
        i1 = jnp.take_along_axis(top_i[:, :, 0], best_f // PEER_TOPK, axis=-1)
        i2 = jnp.take_along_axis(top_i[:, :, 1], best_f % PEER_TOPK, axis=-1)
        idx = i1 * N_KEYS + i2
        gate = jax.nn.softmax(best_v, axis=-1)
        u = expert_u[idx]
        act = jax.nn.gelu(jnp.einsum('thkd,td->thk', u, xc).astype(jnp.float32), approximate=False)
        w = (gate * act).astype(xc.dtype)
        vv = expert_v[idx]
        return jnp.einsum('thk,thkd->td', w, vv)

    return lax.map(chunk_step, chunks).reshape(b, s, d)


def setup_inputs(seed: int = 0) -> dict:
    key = jax.random.key(seed)
    ks = jax.random.split(key, 24)
    L, D, A, C = DEPTH, D_MODEL, ATTN_WIDTH, CONV_WIDTH
    f32 = jnp.float32
    nrm = lambda k, shape, sc: jax.random.normal(k, shape, f32) * sc
    col_scale = jnp.concatenate([jnp.ones((2 * A,), f32), jnp.full((A,), DEEPNORM_BETA, f32),
                                 jnp.ones((2 * C,), f32)])
    return {
        "x": nrm(ks[0], (BATCH, SEQ, D), 1.0),
        "c": nrm(ks[1], (BATCH, D), 1.0),
        "w_ada": nrm(ks[2], (L, D, 6 * D), D ** -0.5),
        "b_ada": nrm(ks[3], (L, 6 * D), 0.01),
        "w_in": nrm(ks[4], (L, D, IN_WIDTH), D ** -0.5) * col_scale,
        "b_in": nrm(ks[5], (L, IN_WIDTH), 0.01),
        "rel_pos_bias": nrm(ks[6], (L, ATTN_HEADS, 2 * WIN_H - 1, 2 * WIN_W - 1), 0.1),
        "conv_w": nrm(ks[7], (L, CONV_K, C), CONV_K ** -0.5),
        "conv_b": nrm(ks[8], (L, C), 0.01),
        "conv_ln_g": 1.0 + nrm(ks[9], (L, C), 0.01),
        "conv_ln_b": nrm(ks[10], (L, C), 0.01),
        "w_out": nrm(ks[11], (L, MIX_WIDTH, D), MIX_WIDTH ** -0.5) * DEEPNORM_BETA,
        "b_out": nrm(ks[12], (L, D), 0.01),
        "ln1_g": 1.0 + nrm(ks[13], (L, D), 0.01),
        "ln1_b": nrm(ks[14], (L, D), 0.01),
        "w_query": nrm(ks[15], (L, D, PEER_HEADS * PEER_QDIM), D ** -0.5),
        "sub_keys": nrm(ks[16], (L, PEER_HEADS, 2, N_KEYS, PEER_HALF), PEER_HALF ** -0.5),
        "expert_u": nrm(ks[17], (L, N_EXPERTS, D), D ** -0.5),
        "expert_v": nrm(ks[18], (L, N_EXPERTS, D), DEEPNORM_BETA),
        "ln2_g": 1.0 + nrm(ks[19], (L, D), 0.01),
        "ln2_b": nrm(ks[20], (L, D), 0.01),
    }


def reference(x, c, w_ada, b_ada, w_in, b_in, rel_pos_bias, conv_w, conv_b, conv_ln_g, conv_ln_b,
              w_out, b_out, ln1_g, ln1_b, w_query, sub_keys, expert_u, expert_v, ln2_g, ln2_b):
    b, s, d = x.shape
    A = ATTN_WIDTH
    for l in range(DEPTH):
        mod = jax.nn.silu(c) @ w_ada[l] + b_ada[l]
        shift1, scale1, gate1, shift2, scale2, gate2 = [m[:, None, :] for m in jnp.split(mod, 6, axis=-1)]
        h = _layer_norm(x) * (1.0 + scale1) + shift1
        z = h @ w_in[l] + b_in[l]
        q = z[..., 0:A].reshape(b, s, ATTN_HEADS, HEAD_DIM)
        k = z[..., A:2 * A].reshape(b, s, ATTN_HEADS, HEAD_DIM)
        v = z[..., 2 * A:3 * A].reshape(b, s, ATTN_HEADS, HEAD_DIM)
        ca = z[..., 3 * A:3 * A + CONV_WIDTH]
        cg = z[..., 3 * A + CONV_WIDTH:]
        y_attn = _neighbourhood_attention(q, k, v, rel_pos_bias[l])
        y_conv = _conformer_conv(ca, cg, conv_w[l], conv_b[l], conv_ln_g[l], conv_ln_b[l])
        y = jnp.concatenate([y_attn, y_conv], axis=-1) @ w_out[l] + b_out[l]
        x = _layer_norm(DEEPNORM_ALPHA * x + gate1 * y, ln1_g[l], ln1_b[l])
        h = _layer_norm(x) * (1.0 + scale2) + shift2
        y = _peer(h, w_query[l], sub_keys[l], expert_u[l], expert_v[l])
        x = _layer_norm(DEEPNORM_ALPHA * x + gate2 * y, ln2_g[l], ln2_b[l])
    return x
```

```python
import functools

import jax
import jax.numpy as jnp
import numpy as np
from jax import lax
from jax.experimental import pallas as pl
from jax.experimental.pallas import tpu as pltpu

F32 = jnp.float32
BF16 = jnp.bfloat16

GRID_W = 64
WIN_H = 8
WIN_W = 16
ATTN_HEADS = 8
HEAD_DIM = 64
CONV_K = 31
PEER_HEADS = 8
PEER_HALF = 128
N_KEYS = 128
PEER_TOPK = 16
LN_EPS = 1e-5
MASK_VALUE = -1e30

LANES = 128
SUBLANES = 8
VMEM_LIMIT = 48 * 1024 * 1024
PEER_VMEM_LIMIT = 56 * 1024 * 1024

TM = 512
TR = 256
TB = 128
P_STRIDE = 136


def _layer_norm_rows(x):
    mu = jnp.mean(x, axis=-1, keepdims=True)
    xc = x - mu
    var = jnp.mean(xc * xc, axis=-1, keepdims=True)
    return xc * lax.rsqrt(var + LN_EPS)


def _ada_kernel(c_ref, w_ref, b_ref, o_ref):
    c = c_ref[...]
    a = c * jax.nn.sigmoid(c)
    o_ref[...] = jnp.dot(a.astype(BF16), w_ref[...].astype(BF16), preferred_element_type=F32) + b_ref[...]


def _ada(c, w, b):
    bsz, d = c.shape
    n = w.shape[1]
    tn = 1024
    return pl.pallas_call(
        _ada_kernel,
        grid=(n // tn,),
        in_specs=[pl.BlockSpec((bsz, d), lambda j: (0, 0)),
                  pl.BlockSpec((d, tn), lambda j: (0, j)),
                  pl.BlockSpec((1, tn), lambda j: (0, j))],
        out_specs=pl.BlockSpec((bsz, tn), lambda j: (0, j)),
        out_shape=jax.ShapeDtypeStruct((bsz, n), F32),
        compiler_params=pltpu.CompilerParams(dimension_semantics=("arbitrary",), vmem_limit_bytes=VMEM_LIMIT),
        name="ada",
    )(c, w, b.reshape(1, n))


def _in_kernel(x_ref, mod_ref, w_ref, b_ref, q_ref, k_ref, v_ref, u_ref, *, a_w, c_w):
    x = x_ref[0]
    shift = mod_ref[0, 0:1, :]
    scale = mod_ref[0, 1:2, :]
    h = _layer_norm_rows(x) * (1.0 + scale) + shift
    z = jnp.dot(h.astype(BF16), w_ref[...], preferred_element_type=F32) + b_ref[...]
    q_ref[0] = (z[:, 0:a_w] * (HEAD_DIM ** -0.5)).astype(BF16)
    k_ref[0] = z[:, a_w:2 * a_w].astype(BF16)
    v_ref[0] = z[:, 2 * a_w:3 * a_w].astype(BF16)
    a = z[:, 3 * a_w:3 * a_w + c_w]
    g = z[:, 3 * a_w + c_w:3 * a_w + 2 * c_w]
    u_ref[0] = a * jax.nn.sigmoid(g)


def _in_proj(x, mod3, w_in_bf, b_in, a_w, c_w):
    bsz, s, d = x.shape
    n = w_in_bf.shape[1]
    return pl.pallas_call(
        functools.partial(_in_kernel, a_w=a_w, c_w=c_w),
        grid=(bsz, s // TM),
        in_specs=[pl.BlockSpec((1, TM, d), lambda b, i: (b, i, 0)),
                  pl.BlockSpec((1, 6, d), lambda b, i: (b, 0, 0)),
                  pl.BlockSpec((d, n), lambda b, i: (0, 0)),
                  pl.BlockSpec((1, n), lambda b, i: (0, 0))],
        out_specs=[pl.BlockSpec((1, TM, a_w), lambda b, i: (b, i, 0)),
                   pl.BlockSpec((1, TM, a_w), lambda b, i: (b, i, 0)),
                   pl.BlockSpec((1, TM, a_w), lambda b, i: (b, i, 0)),
                   pl.BlockSpec((1, TM, c_w), lambda b, i: (b, i, 0))],
        out_shape=[jax.ShapeDtypeStruct((bsz, s, a_w), BF16),
                   jax.ShapeDtypeStruct((bsz, s, a_w), BF16),
                   jax.ShapeDtypeStruct((bsz, s, a_w), BF16),
                   jax.ShapeDtypeStruct((bsz, s, c_w), F32)],
        compiler_params=pltpu.CompilerParams(dimension_semantics=("arbitrary", "arbitrary"),
                                             vmem_limit_bytes=VMEM_LIMIT),
        name="in_proj",
    )(x, mod3, w_in_bf, b_in.reshape(1, n))


def _attn_bias_table(rpb):
    qcol = np.arange(GRID_W)
    kcol = np.arange(GRID_W)
    col_start = np.clip(qcol - WIN_W // 2, 0, GRID_W - WIN_W)
    valid = (kcol[None, :] >= col_start[:, None]) & (kcol[None, :] < col_start[:, None] + WIN_W)
    dc = np.clip(kcol[None, :] - qcol[:, None] + WIN_W - 1, 0, 2 * WIN_W - 2)
    dr = np.arange(WIN_H)[:, None] + np.arange(WIN_H)[None, :]
    t = rpb[:, dr][:, :, :, dc]
    t = jnp.where(valid[None, None, None], t.astype(F32), MASK_VALUE)
    t = t.transpose(0, 1, 3, 2, 4)
    h = rpb.shape[0]
    return t.reshape(h, WIN_H, GRID_W, WIN_H * GRID_W)


def _attn_kernel(q_ref, k_ref, v_ref, bias_ref, o_ref, *, rows):
    kblk = WIN_H * GRID_W
    for hh in range(LANES // HEAD_DIM):
        lo = hh * HEAD_DIM

        def row_step(r, carry, hh=hh, lo=lo):
            rs = jnp.clip(r - WIN_H // 2, 0, rows - WIN_H)
            var = rs - r + WIN_H - 1
            q0 = pl.multiple_of(r * GRID_W, GRID_W)
            k0 = pl.multiple_of(rs * GRID_W, GRID_W)
            q = q_ref[0, pl.ds(q0, GRID_W), lo:lo + HEAD_DIM]
            kb = k_ref[0, pl.ds(k0, kblk), lo:lo + HEAD_DIM]
            vb = v_ref[0, pl.ds(k0, kblk), lo:lo + HEAD_DIM]
            s = lax.dot_general(q, kb, (((1,), (1,)), ((), ())), preferred_element_type=F32)
            s = s + bias_ref[hh, var]
            m = jnp.max(s, axis=-1, keepdims=True)
            e = jnp.exp(s - m)
            l = jnp.sum(e, axis=-1, keepdims=True)
            o = jnp.dot(e.astype(BF16), vb, preferred_element_type=F32) / l
            o_ref[0, pl.ds(q0, GRID_W), lo:lo + HEAD_DIM] = o.astype(o_ref.dtype)
            return carry

        lax.fori_loop(0, rows, row_step, 0)


def _attention(q, k, v, bias_tbl):
    bsz, s, a_w = q.shape
    rows = s // GRID_W
    assert rows >= WIN_H and s % GRID_W == 0
    hp = LANES // HEAD_DIM
    blk = pl.BlockSpec((1, s, LANES), lambda b, g: (b, 0, g))
    return pl.pallas_call(
        functools.partial(_attn_kernel, rows=rows),
        grid=(bsz, a_w // LANES),
        in_specs=[blk, blk, blk,
                  pl.BlockSpec((hp, WIN_H, GRID_W, WIN_H * GRID_W), lambda b, g: (g, 0, 0, 0))],
        out_specs=blk,
        out_shape=jax.ShapeDtypeStruct((bsz, s, a_w), BF16),
        compiler_params=pltpu.CompilerParams(dimension_semantics=("arbitrary", "arbitrary"),
                                             vmem_limit_bytes=VMEM_LIMIT),
        name="attn",
    )(q, k, v, bias_tbl)


CONV_PAD = 16
CONV_TS = 128


def _conv_kernel(u_ref, w_ref, cb_ref, g_ref, b_ref, o_ref, pad_ref, y_ref, *, s):
    c_w = u_ref.shape[2]
    zeros = jnp.zeros((CONV_PAD, c_w), F32)
    pad_ref[0:CONV_PAD, :] = zeros
    pad_ref[CONV_PAD + s:CONV_PAD + s + CONV_PAD, :] = zeros
    pad_ref[CONV_PAD:CONV_PAD + s, :] = u_ref[0]
    off = CONV_PAD - CONV_K // 2

    for ct in range(c_w // LANES):
        c0 = ct * LANES

        def tile(i, carry, c0=c0):
            s0 = pl.multiple_of(i * CONV_TS, CONV_TS)
            win = pad_ref[pl.ds(s0, CONV_TS + 2 * CONV_PAD), c0:c0 + LANES]
            acc = jnp.zeros((CONV_TS, LANES), F32)
            for kk in range(CONV_K):
                acc = acc + win[off + kk:off + kk + CONV_TS] * w_ref[kk:kk + 1, c0:c0 + LANES]
            y_ref[pl.ds(s0, CONV_TS), c0:c0 + LANES] = acc + cb_ref[0:1, c0:c0 + LANES]
            return carry

        lax.fori_loop(0, s // CONV_TS, tile, 0)

    def norm(i, carry):
        s0 = pl.multiple_of(i * CONV_TS, CONV_TS)
        y = _layer_norm_rows(y_ref[pl.ds(s0, CONV_TS), :]) * g_ref[...] + b_ref[...]
        o_ref[0, pl.ds(s0, CONV_TS), :] = (y * jax.nn.sigmoid(y)).astype(o_ref.dtype)
        return carry

    lax.fori_loop(0, s // CONV_TS, norm, 0)


def _conv(u, conv_w, conv_b, ln_g, ln_b):
    bsz, s, c_w = u.shape
    vec = lambda a: a.reshape(1, c_w)
    cvec = pl.BlockSpec((1, c_w), lambda b: (0, 0))
    return pl.pallas_call(
        functools.partial(_conv_kernel, s=s),
        grid=(bsz,),
        in_specs=[pl.BlockSpec((1, s, c_w), lambda b: (b, 0, 0)),
                  pl.BlockSpec((CONV_K, c_w), lambda b: (0, 0)), cvec, cvec, cvec],
        out_specs=pl.BlockSpec((1, s, c_w), lambda b: (b, 0, 0)),
        out_shape=jax.ShapeDtypeStruct((bsz, s, c_w), BF16),
        scratch_shapes=[pltpu.VMEM((s + 2 * CONV_PAD, c_w), F32), pltpu.VMEM((s, c_w), F32)],
        compiler_params=pltpu.CompilerParams(dimension_semantics=("arbitrary",), vmem_limit_bytes=VMEM_LIMIT),
        name="conv",
    )(u, conv_w, vec(conv_b), vec(ln_g), vec(ln_b))


def _out_kernel(ya_ref, yc_ref, x_ref, mod_ref, wa_ref, wc_ref, bo_ref, g_ref, b_ref, x1_ref, h2_ref, *, alpha):
    y = jnp.dot(ya_ref[0], wa_ref[...], preferred_element_type=F32)
    y = y + jnp.dot(yc_ref[0], wc_ref[...], preferred_element_type=F32) + bo_ref[...]
    gate1 = mod_ref[0, 2:3, :]
    shift2 = mod_ref[0, 3:4, :]
    scale2 = mod_ref[0, 4:5, :]
    x1 = _layer_norm_rows(alpha * x_ref[0] + gate1 * y) * g_ref[...] + b_ref[...]
    x1_ref[0] = x1
    h2_ref[0] = (_layer_norm_rows(x1) * (1.0 + scale2) + shift2).astype(BF16)


def _out_proj(ya, yc, x, mod3, wa_bf, wc_bf, b_out, ln_g, ln_b, alpha):
    bsz, s, d = x.shape
    a_w, c_w = ya.shape[2], yc.shape[2]
    dvec = pl.BlockSpec((1, d), lambda b, i: (0, 0))
    vec = lambda a: a.reshape(1, d)
    return pl.pallas_call(
        functools.partial(_out_kernel, alpha=alpha),
        grid=(bsz, s // TM),
        in_specs=[pl.BlockSpec((1, TM, a_w), lambda b, i: (b, i, 0)),
                  pl.BlockSpec((1, TM, c_w), lambda b, i: (b, i, 0)),
                  pl.BlockSpec((1, TM, d), lambda b, i: (b, i, 0)),
                  pl.BlockSpec((1, 6, d), lambda b, i: (b, 0, 0)),
                  pl.BlockSpec((a_w, d), lambda b, i: (0, 0)),
                  pl.BlockSpec((c_w, d), lambda b, i: (0, 0)),
                  dvec, dvec, dvec],
        out_specs=[pl.BlockSpec((1, TM, d), lambda b, i: (b, i, 0)),
                   pl.BlockSpec((1, TM, d), lambda b, i: (b, i, 0))],
        out_shape=[jax.ShapeDtypeStruct((bsz, s, d), F32), jax.ShapeDtypeStruct((bsz, s, d), BF16)],
        compiler_params=pltpu.CompilerParams(dimension_semantics=("arbitrary", "arbitrary"),
                                             vmem_limit_bytes=VMEM_LIMIT),
        name="out_proj",
    )(ya, yc, x, mod3, wa_bf, wc_bf, vec(b_out), vec(ln_g), vec(ln_b))


def _top_rows(s, f, k, payload=None):
    big = jnp.int32(1 << 30)
    vals, outs = [], []
    for _ in range(k):
        m = jnp.max(s, axis=0, keepdims=True)
        fi = jnp.min(jnp.where(s == m, f, big), axis=0, keepdims=True)
        sel = f == fi
        vals.append(m)
        if payload is None:
            outs.append(fi)
        else:
            outs.append(jnp.max(jnp.where(sel, payload, -1), axis=0, keepdims=True))
        s = jnp.where(sel, -jnp.inf, s)
    return jnp.concatenate(vals, axis=0), jnp.concatenate(outs, axis=0)


def _route_kernel(h_ref, wq_ref, keys_ref, idx_ref, gate_ref):
    t = h_ref.shape[0]
    k = PEER_TOPK
    q = jnp.dot(h_ref[...], wq_ref[...], preferred_element_type=F32).astype(BF16)
    key_iota = lax.broadcasted_iota(jnp.int32, (N_KEYS, t), 0)

    n_cand = k + (k // 2 - 1) * (k // 2) + k // 2
    r = lax.broadcasted_iota(jnp.int32, (n_cand, t), 0)
    mid = r - k
    f_tag = jnp.where(r < k, r, jnp.where(r < n_cand - k // 2,
                                          ((mid // (k // 2)) + 1) * k + mid % (k // 2),
                                          (r - (n_cand - k // 2) + k // 2) * k))

    idx_rows, gate_rows = [], []
    for h in range(PEER_HEADS):
        tops = []
        for p in range(2):
            c0 = (h * 2 + p) * PEER_HALF
            sc = lax.dot_general(keys_ref[h * 2 + p], q[:, c0:c0 + PEER_HALF],
                                 (((1,), (1,)), ((), ())), preferred_element_type=F32)
            tops.append(_top_rows(sc, key_iota, k))
        (v1, i1), (v2, i2) = tops
        e1 = i1 * N_KEYS
        cand = [v1[0:1] + v2]
        eidx = [e1[0:1] + i2]
        for a in range(1, k // 2):
            cand.append(v1[a:a + 1] + v2[0:k // 2])
            eidx.append(e1[a:a + 1] + i2[0:k // 2])
        cand.append(v1[k // 2:k] + v2[0:1])
        eidx.append(e1[k // 2:k] + i2[0:1])
        best_v, best_e = _top_rows(jnp.concatenate(cand, axis=0), f_tag, k,
                                   payload=jnp.concatenate(eidx, axis=0))
        ex = jnp.exp(best_v - best_v[0:1])
        gate_rows.append(ex / jnp.sum(ex, axis=0, keepdims=True))
        idx_rows.append(best_e)
    idx_ref[...] = jnp.concatenate(idx_rows, axis=0).T
    gate_ref[...] = jnp.concatenate(gate_rows, axis=0).T


def _route(h2, wq_bf, keys_bf):
    n, d = h2.shape
    nq = wq_bf.shape[1]
    hk = PEER_HEADS * PEER_TOPK
    return pl.pallas_call(
        _route_kernel,
        grid=(n // TR,),
        in_specs=[pl.BlockSpec((TR, d), lambda i: (i, 0)),
                  pl.BlockSpec((d, nq), lambda i: (0, 0)),
                  pl.BlockSpec((2 * PEER_HEADS, N_KEYS, PEER_HALF), lambda i: (0, 0, 0))],
        out_specs=[pl.BlockSpec((TR, hk), lambda i: (i, 0)), pl.BlockSpec((TR, hk), lambda i: (i, 0))],
        out_shape=[jax.ShapeDtypeStruct((n, hk), jnp.int32), jax.ShapeDtypeStruct((n, hk), F32)],
        compiler_params=pltpu.CompilerParams(dimension_semantics=("arbitrary",), vmem_limit_bytes=VMEM_LIMIT),
        name="route",
    )(h2, wq_bf, keys_bf)


def _pack_table(tbl):
    e, d = tbl.shape
    half = d // 2
    bits = lax.bitcast_convert_type(tbl.astype(BF16), jnp.uint16).astype(jnp.uint32)
    packed = (bits[:, :half] << 16) | bits[:, half:]
    return packed.reshape(e, half // LANES, LANES)


def _unpack(words):
    hi = pltpu.bitcast(words & jnp.uint32(0xFFFF0000), F32)
    lo = pltpu.bitcast(words << 16, F32)
    return hi, lo


def _peer_u_kernel(idx_ref, x_ref, gate_ref, tbl_ref, w_ref, p_ref, *, nsel, rows_per):
    def token(t, carry):
        xt = x_ref[t]
        xa = xt[0:rows_per]
        xb = xt[rows_per:2 * rows_per]
        for j in range(nsel):
            hi, lo = _unpack(tbl_ref[idx_ref[t, j]])
            p_ref[pl.ds(j, rows_per, stride=P_STRIDE), :] = hi * xa + lo * xb
        pm = p_ref[0:nsel, :]
        for c in range(1, rows_per):
            pm = pm + p_ref[c * P_STRIDE:c * P_STRIDE + nsel, :]
        act = jnp.sum(pm.T, axis=0, keepdims=True)
        gelu = 0.5 * act * (1.0 + lax.erf(act * (2.0 ** -0.5)))
        w_ref[t] = (gate_ref[t] * gelu).astype(BF16).astype(F32)
        return carry

    lax.fori_loop(0, x_ref.shape[0], token, 0)


def _peer_u(idx, x3, gate3, tbl):
    n, nsel = idx.shape
    e, rows_per, _ = tbl.shape
    assert nsel == LANES and 2 * rows_per == x3.shape[1]
    return pl.pallas_call(
        functools.partial(_peer_u_kernel, nsel=nsel, rows_per=rows_per),
        grid=(n // TB,),
        in_specs=[pl.BlockSpec((TB, nsel), lambda i: (i, 0), memory_space=pltpu.SMEM),
                  pl.BlockSpec((TB, 2 * rows_per, LANES), lambda i: (i, 0, 0)),
                  pl.BlockSpec((TB, 1, nsel), lambda i: (i, 0, 0)),
                  pl.BlockSpec((e, rows_per, LANES), lambda i: (0, 0, 0), pipeline_mode=pl.Buffered(1))],
        out_specs=pl.BlockSpec((TB, 1, nsel), lambda i: (i, 0, 0)),
        out_shape=jax.ShapeDtypeStruct((n, 1, nsel), F32),
        scratch_shapes=[pltpu.VMEM((rows_per * P_STRIDE, LANES), F32)],
        compiler_params=pltpu.CompilerParams(dimension_semantics=("arbitrary",), vmem_limit_bytes=PEER_VMEM_LIMIT),
        name="peer_u",
    )(idx, x3, gate3, tbl)


def _peer_v_kernel(idx_ref, w_ref, x1_ref, g2_ref, lg_ref, lb_ref, tbl_ref, o_ref, g_ref, *, nsel, rows_per, alpha):
    inv_d = 1.0 / (2 * rows_per * LANES)

    def token(t, carry):
        for j in range(nsel):
            g_ref[pl.ds(j, rows_per, stride=P_STRIDE), :] = tbl_ref[idx_ref[t, j]]
        wb = jnp.broadcast_to(w_ref[t], (nsel, nsel)).T
        his, los = [], []
        for c in range(rows_per):
            hi, lo = _unpack(g_ref[c * P_STRIDE:c * P_STRIDE + nsel, :])
            his.append(jnp.sum(wb * hi, axis=0, keepdims=True))
            los.append(jnp.sum(wb * lo, axis=0, keepdims=True))
        y = jnp.concatenate(his + los, axis=0)
        z = alpha * x1_ref[t] + g2_ref[0] * y
        mu = jnp.sum(z, keepdims=True) * inv_d
        zc = z - mu
        var = jnp.sum(zc * zc, keepdims=True) * inv_d
        o_ref[t] = zc * lax.rsqrt(var + LN_EPS) * lg_ref[...] + lb_ref[...]
        return carry

    lax.fori_loop(0, o_ref.shape[0], token, 0)


def _peer_v(idx, w2, x1_3, gate2_3, ln_g3, ln_b3, tbl, tokens_per_batch, alpha):
    n, nsel = idx.shape
    e, rows_per, _ = tbl.shape
    r8 = 2 * rows_per
    steps_per_batch = tokens_per_batch // TB
    pvec = pl.BlockSpec((r8, LANES), lambda i: (0, 0))
    return pl.pallas_call(
        functools.partial(_peer_v_kernel, nsel=nsel, rows_per=rows_per, alpha=alpha),
        grid=(n // TB,),
        in_specs=[pl.BlockSpec((TB, nsel), lambda i: (i, 0), memory_space=pltpu.SMEM),
                  pl.BlockSpec((TB, 1, nsel), lambda i: (i, 0, 0)),
                  pl.BlockSpec((TB, r8, LANES), lambda i: (i, 0, 0)),
                  pl.BlockSpec((1, r8, LANES), lambda i: (i // steps_per_batch, 0, 0)),
                  pvec, pvec,
                  pl.BlockSpec((e, rows_per, LANES), lambda i: (0, 0, 0), pipeline_mode=pl.Buffered(1))],
        out_specs=pl.BlockSpec((TB, r8, LANES), lambda i: (i, 0, 0)),
        out_shape=jax.ShapeDtypeStruct((n, r8, LANES), F32),
        scratch_shapes=[pltpu.VMEM((rows_per * P_STRIDE, LANES), jnp.uint32)],
        compiler_params=pltpu.CompilerParams(dimension_semantics=("arbitrary",), vmem_limit_bytes=PEER_VMEM_LIMIT),
        name="peer_v",
    )(idx, w2, x1_3, gate2_3, ln_g3, ln_b3, tbl)


def kernel(x, c, w_ada, b_ada, w_in, b_in, rel_pos_bias, conv_w, conv_b, conv_ln_g, conv_ln_b,
           w_out, b_out, ln1_g, ln1_b, w_query, sub_keys, expert_u, expert_v, ln2_g, ln2_b):
    bsz, s, d = x.shape
    depth = w_ada.shape[0]
    a_w = ATTN_HEADS * HEAD_DIM
    c_w = d - a_w
    n = bsz * s
    r8 = d // LANES
    assert s % TM == 0 and n % TR == 0 and s % TB == 0 and r8 == SUBLANES
    alpha = (2.0 * depth) ** 0.25

    for l in range(depth):
        mod3 = _ada(c, w_ada[l], b_ada[l]).reshape(bsz, 6, d)
        q, k, v, u = _in_proj(x, mod3, w_in[l].astype(BF16), b_in[l], a_w, c_w)
        y_attn = _attention(q, k, v, _attn_bias_table(rel_pos_bias[l]))
        y_conv = _conv(u, conv_w[l], conv_b[l], conv_ln_g[l], conv_ln_b[l])
        w_out_bf = w_out[l].astype(BF16)
        x1, h2 = _out_proj(y_attn, y_conv, x, mod3, w_out_bf[:a_w], w_out_bf[a_w:], b_out[l],
                           ln1_g[l], ln1_b[l], alpha)

        h2 = h2.reshape(n, d)
        keys_bf = sub_keys[l].astype(BF16).reshape(2 * PEER_HEADS, N_KEYS, PEER_HALF)
        idx, gate = _route(h2, w_query[l].astype(BF16), keys_bf)
        nsel = idx.shape[1]

        x3 = h2.astype(F32).reshape(n, r8, LANES)
        w_sel = _peer_u(idx, x3, gate.reshape(n, 1, nsel), _pack_table(expert_u[l]))
        gate2_3 = mod3[:, 5, :].reshape(bsz, r8, LANES)
        x = _peer_v(idx, w_sel, x1.reshape(n, r8, LANES), gate2_3,
                    ln2_g[l].reshape(r8, LANES), ln2_b[l].reshape(r8, LANES),
                    _pack_table(expert_v[l]), s, alpha).reshape(bsz, s, d)
    return x
```
